```python
import math
import jax
import jax.numpy as jnp
from jax import lax
import numpy as np

D_MODEL = 1024
BATCH = 4
SEQ = 4096
DEPTH = 1

N_META = 16
DN_HEAD_DIM = 64
DN_WIDTH = D_MODEL // 2
DN_HEADS = DN_WIDTH // DN_HEAD_DIM
CONV_K = 4
CHUNK = 64
S_CH = 16
S_WIDTH = D_MODEL // 2
S_GROUPS = S_WIDTH // S_CH
S_STATE = 64
MOE_GROUPS = 8
EXPERTS_PER_GROUP = 8
N_EXPERTS = MOE_GROUPS * EXPERTS_PER_GROUP
TOP_K = 2
D_EXPERT = D_MODEL // 4
MOE_BLOCK = 128
DEEP_ALPHA = (2 * DEPTH) ** 0.25
DEEP_BETA = (8 * DEPTH) ** -0.25
LN_EPS = 1e-5
NORM_EPS = 1e-6
SPLIT_SIZES = (DN_WIDTH, DN_WIDTH, DN_WIDTH, DN_WIDTH, S_WIDTH, D_MODEL, D_MODEL, DN_HEADS, DN_HEADS)
SPLIT_POINTS = tuple(int(s) for s in np.cumsum(SPLIT_SIZES)[:-1])
IN_WIDTH = int(sum(SPLIT_SIZES))

kernel_name = 'hybrid_deltanet_s5_hmoe_block'


def layer_norm(x, g, b):
    xf = x.astype(jnp.float32)
    mu = jnp.mean(xf, -1, keepdims=True)
    xc = xf - mu
    var = jnp.mean(xc * xc, -1, keepdims=True)
    y = xc * lax.rsqrt(var + LN_EPS) * g.astype(jnp.float32) + b.astype(jnp.float32)
    return y.astype(x.dtype)


def l2_normalize(t):
    return t * lax.rsqrt(jnp.sum(t * t, -1, keepdims=True) + NORM_EPS)


def causal_depthwise_conv(x, w):
    k, c = w.shape
    return lax.conv_general_dilated(x, w[:, None, :].astype(x.dtype), window_strides=(1,), padding=[(k - 1, 0)],
                                    dimension_numbers=('NWC', 'WIO', 'NWC'), feature_group_count=c)


def gated_deltanet(q, k, v, z, a, b, conv_w, a_log, dt_bias, norm_w):
    out_dtype = z.dtype
    f32 = jnp.float32
    bn, ln, _ = q.shape
    qkv = jax.nn.silu(causal_depthwise_conv(jnp.concatenate([q, k, v], -1), conv_w)).astype(f32)
    q, k, v = jnp.split(qkv, 3, axis=-1)
    q = l2_normalize(q.reshape(bn, ln, DN_HEADS, DN_HEAD_DIM)) * (DN_HEAD_DIM ** -0.5)
    k = l2_normalize(k.reshape(bn, ln, DN_HEADS, DN_HEAD_DIM))
    v = v.reshape(bn, ln, DN_HEADS, DN_HEAD_DIM)
    beta = jax.nn.sigmoid(b.astype(f32))
    g = -jnp.exp(a_log.astype(f32)) * jax.nn.softplus(a.astype(f32) + dt_bias.astype(f32))
    pad = (-N_META) % CHUNK
    lp = ln + pad
    nc = lp // CHUNK

    def to_chunks(t):
        t = jnp.pad(t, [(0, 0), (pad, 0)] + [(0, 0)] * (t.ndim - 2))
        t = t.reshape((bn, nc, CHUNK) + t.shape[2:])
        return jnp.moveaxis(t, 3, 1)

    q, k, v, beta, g = (to_chunks(t) for t in (q, k, v, beta, g))
    gc = jnp.cumsum(g, axis=-1)
    incl = jnp.tril(jnp.ones((CHUNK, CHUNK), bool))
    strict = jnp.tril(jnp.ones((CHUNK, CHUNK), bool), -1)
    decay = jnp.exp(jnp.where(incl, gc[..., :, None] - gc[..., None, :], -jnp.inf))
    kb = k * beta[..., None]
    a_mat = jnp.where(strict, jnp.einsum('bhnid,bhnjd->bhnij', kb, k) * decay, 0.0)
    eye = jnp.eye(CHUNK, dtype=f32)
    rhs = jnp.concatenate([v * beta[..., None], kb * jnp.exp(gc)[..., None]], -1)
    sol = lax.linalg.triangular_solve(a_mat + eye, rhs, left_side=True, lower=True, unit_diagonal=True)
    u_c, w_c = sol[..., :DN_HEAD_DIM], sol[..., DN_HEAD_DIM:]
    attn = jnp.einsum('bhnid,bhnjd->bhnij', q, k) * decay
    q_dec = q * jnp.exp(gc)[..., None]
    k_dec = k * jnp.exp(gc[..., -1:] - gc)[..., None]
    g_last = jnp.exp(gc[..., -1])

    def step(state, inp):
        qd, kd, uc, wc, ac, gl = inp
        v_new = uc - jnp.einsum('bhck,bhkv->bhcv', wc, state)
        out = jnp.einsum('bhck,bhkv->bhcv', qd, state) + jnp.einsum('bhij,bhjv->bhiv', ac, v_new)
        state = state * gl[..., None, None] + jnp.einsum('bhck,bhcv->bhkv', kd, v_new)
        return state, out

    xs = tuple(jnp.moveaxis(t, 2, 0) for t in (q_dec, k_dec, u_c, w_c, attn, g_last))
    s0 = jnp.zeros((bn, DN_HEADS, DN_HEAD_DIM, DN_HEAD_DIM), f32)
    _, o = lax.scan(step, s0, xs)
    o = jnp.transpose(o, (1, 0, 3, 2, 4)).reshape(bn, lp, DN_HEADS, DN_HEAD_DIM)[:, pad:]
    o = o * lax.rsqrt(jnp.mean(o * o, -1, keepdims=True) + NORM_EPS) * norm_w.astype(f32)
    o = o * jax.nn.silu(z.astype(f32).reshape(bn, ln, DN_HEADS, DN_HEAD_DIM))
    return o.reshape(bn, ln, DN_WIDTH).astype(out_dtype)


def s5_ssm(u, a_re, a_im, log_dt, b_re, b_im, c_re, c_im, d_skip):
    f32 = jnp.float32
    bn, ln, _ = u.shape
    uf = u.astype(f32).reshape(bn, ln, S_GROUPS, S_CH)
    ar, ai = a_re.astype(f32), a_im.astype(f32)
    dt = jnp.exp(log_dt.astype(f32))[:, None]
    mag = jnp.exp(dt * ar)
    abar_re, abar_im = mag * jnp.cos(dt * ai), mag * jnp.sin(dt * ai)
    den = ar * ar + ai * ai
    nr, ni = abar_re - 1.0, abar_im
    f_re, f_im = (nr * ar + ni * ai) / den, (ni * ar - nr * ai) / den
    br, bi = b_re.astype(f32), b_im.astype(f32)
    bb_re = f_re[..., None] * br - f_im[..., None] * bi
    bb_im = f_re[..., None] * bi + f_im[..., None] * br
    bu_re = jnp.einsum('blgc,gpc->blgp', uf, bb_re)
    bu_im = jnp.einsum('blgc,gpc->blgp', uf, bb_im)
    a_re_t = jnp.broadcast_to(abar_re, bu_re.shape)
    a_im_t = jnp.broadcast_to(abar_im, bu_re.shape)

    def combine(e1, e2):
        a1r, a1i, b1r, b1i = e1
        a2r, a2i, b2r, b2i = e2
        return (a2r * a1r - a2i * a1i, a2r * a1i + a2i * a1r,
                a2r * b1r - a2i * b1i + b2r, a2r * b1i + a2i * b1r + b2i)

    _, _, h_re, h_im = lax.associative_scan(combine, (a_re_t, a_im_t, bu_re, bu_im), axis=1)
    y = (jnp.einsum('blgp,gcp->blgc', h_re, c_re.astype(f32))
         - jnp.einsum('blgp,gcp->blgc', h_im, c_im.astype(f32))
         + d_skip.astype(f32).reshape(S_GROUPS, S_CH) * uf)
    return y.reshape(bn, ln, S_WIDTH)


def hierarchical_moe(h, w_rg, b_rg, w_re, b_re, w1, w3, w2):
    f32 = jnp.float32
    bn, ln, d = h.shape
    t = bn * ln
    xt = h.reshape(t, d)
    g_prob = jax.nn.softmax((xt @ w_rg).astype(f32) + b_rg.astype(f32), axis=-1)
    g_p, g_idx = lax.top_k(g_prob, 1)
    e_logit = ((xt @ w_re).astype(f32) + b_re.astype(f32)).reshape(t, MOE_GROUPS, EXPERTS_PER_GROUP)
    e_logit = jnp.take_along_axis(e_logit, g_idx[:, :, None], axis=1)[:, 0]
    e_top, e_loc = lax.top_k(e_logit, TOP_K)
    gate = jax.nn.softmax(e_top, axis=-1) * g_p
    expert = g_idx * EXPERTS_PER_GROUP + e_loc
    m = t * TOP_K
    e_flat = expert.reshape(m)
    order = jnp.argsort(e_flat)
    e_s = e_flat[order]
    tok_s = order // TOP_K
    w_s = gate.reshape(m)[order]
    counts = jnp.zeros((N_EXPERTS,), jnp.int32).at[e_flat].add(1)
    padded = (counts + MOE_BLOCK - 1) // MOE_BLOCK * MOE_BLOCK
    starts = jnp.cumsum(counts) - counts
    p_ends = jnp.cumsum(padded)
    p_starts = p_ends - padded
    dest = p_starts[e_s] + jnp.arange(m, dtype=jnp.int32) - starts[e_s]
    cap = -(-(m + N_EXPERTS * (MOE_BLOCK - 1)) // MOE_BLOCK) * MOE_BLOCK
    n_blk = cap // MOE_BLOCK
    row_tok = jnp.zeros((cap,), jnp.int32).at[dest].set(tok_s)
    row_w = jnp.zeros((cap,), f32).at[dest].set(w_s)
    blk_expert = jnp.minimum(
        jnp.searchsorted(p_ends, jnp.arange(n_blk, dtype=jnp.int32) * MOE_BLOCK, side='right'), N_EXPERTS - 1)
    xb = xt[row_tok].reshape(n_blk, MOE_BLOCK, d)

    def expert_block(args):
        xblk, e = args
        hid = jax.nn.silu(xblk @ w1[e]) * (xblk @ w3[e])
        return hid @ w2[e]

    yb = lax.map(expert_block, (xb, blk_expert)).reshape(cap, d)
    out = jax.ops.segment_sum(yb * row_w[:, None].astype(yb.dtype), row_tok, num_segments=t)
    return out.reshape(bn, ln, d)


def hybrid_layer(h, w_in, conv_w, dn_a_log, dn_dt_bias, dn_norm_w, ssm_a_re, ssm_a_im, ssm_log_dt,
                 ssm_b_re, ssm_b_im, ssm_c_re, ssm_c_im, ssm_d, w_glu, b_glu, w_branch, w_out,
                 ln1_g, ln1_b, w_rg, b_rg, w_re, b_re, w1, w3, w2, ln2_g, ln2_b):
    proj = h @ w_in
    q, k, v, z, u, gate_a, gate_b, a, b = jnp.split(proj, SPLIT_POINTS, axis=-1)
    o_a = gated_deltanet(q, k, v, z, a, b, conv_w, dn_a_log, dn_dt_bias, dn_norm_w)
    y_b = jax.nn.gelu(s5_ssm(u, ssm_a_re, ssm_a_im, ssm_log_dt, ssm_b_re, ssm_b_im, ssm_c_re, ssm_c_im, ssm_d)).astype(h.dtype)
    o_b = y_b * jax.nn.sigmoid(y_b @ w_glu + b_glu)
    merged = jax.nn.sigmoid(gate_a) * (o_a @ w_branch[0]) + jax.nn.sigmoid(gate_b) * (o_b @ w_branch[1])
    h = layer_norm(DEEP_ALPHA * h + merged @ w_out, ln1_g, ln1_b)
    h = layer_norm(DEEP_ALPHA * h + hierarchical_moe(h, w_rg, b_rg, w_re, b_re, w1, w3, w2), ln2_g, ln2_b)
    return h


def setup_inputs(seed: int = 0) -> dict:
    key = jax.random.key(seed)
    ks = jax.random.split(key, 40)
    f32 = jnp.float32

    def nrm(k, shape, scale):
        return jax.random.normal(k, shape, f32) * scale

    dt_dn = jnp.exp(jax.random.uniform(ks[7], (DEPTH, DN_HEADS), f32, math.log(1e-3), math.log(1e-1)))
    return {
        'x': nrm(ks[0], (BATCH, SEQ, D_MODEL), 1.0),
        'meta': nrm(ks[1], (N_META, D_MODEL), 1.0),
        'ln_emb_g': 1.0 + nrm(ks[2], (D_MODEL,), 0.02),
        'ln_emb_b': nrm(ks[3], (D_MODEL,), 0.02),
        'w_in': nrm(ks[4], (DEPTH, D_MODEL, IN_WIDTH), D_MODEL ** -0.5),
        'conv_w': nrm(ks[5], (DEPTH, CONV_K, 3 * DN_WIDTH), CONV_K ** -0.5),
        'dn_a_log': jnp.log(jax.random.uniform(ks[6], (DEPTH, DN_HEADS), f32, 1.0, 16.0)),
        'dn_dt_bias': dt_dn + jnp.log(-jnp.expm1(-dt_dn)),
        'dn_norm_w': 1.0 + nrm(ks[8], (DEPTH, DN_HEAD_DIM), 0.02),
        'ssm_a_re': -0.5 + nrm(ks[9], (DEPTH, S_GROUPS, S_STATE), 0.01),
        'ssm_a_im': math.pi * jnp.arange(S_STATE, dtype=f32) + nrm(ks[10], (DEPTH, S_GROUPS, S_STATE), 0.01),
        'ssm_log_dt': jax.random.uniform(ks[11], (DEPTH, S_GROUPS), f32, math.log(1e-3), math.log(1e-1)),
        'ssm_b_re': nrm(ks[12], (DEPTH, S_GROUPS, S_STATE, S_CH), (2 * S_CH) ** -0.5),
        'ssm_b_im': nrm(ks[13], (DEPTH, S_GROUPS, S_STATE, S_CH), (2 * S_CH) ** -0.5),
        'ssm_c_re': nrm(ks[14], (DEPTH, S_GROUPS, S_CH, S_STATE), S_STATE ** -0.5),
        'ssm_c_im': nrm(ks[15], (DEPTH, S_GROUPS, S_CH, S_STATE), S_STATE ** -0.5),
        'ssm_d': nrm(ks[16], (DEPTH, S_WIDTH), 1.0),
        'w_glu': nrm(ks[17], (DEPTH, S_WIDTH, S_WIDTH), S_WIDTH ** -0.5),
        'b_glu': nrm(ks[18], (DEPTH, S_WIDTH), 0.02),
        'w_branch': nrm(ks[19], (DEPTH, 2, DN_WIDTH, D_MODEL), DN_WIDTH ** -0.5),
        'w_out': nrm(ks[20], (DEPTH, D_MODEL, D_MODEL), D_MODEL ** -0.5 * DEEP_BETA),
        'ln1_g': 1.0 + nrm(ks[21], (DEPTH, D_MODEL), 0.02),
        'ln1_b': nrm(ks[22], (DEPTH, D_MODEL), 0.02),
        'w_rg': nrm(ks[23], (DEPTH, D_MODEL, MOE_GROUPS), D_MODEL ** -0.5),
        'b_rg': nrm(ks[24], (DEPTH, MOE_GROUPS), 0.01),
        'w_re': nrm(ks[25], (DEPTH, D_MODEL, N_EXPERTS), D_MODEL ** -0.5),
        'b_re': nrm(ks[26], (DEPTH, N_EXPERTS), 0.01),
        'w1': nrm(ks[27], (DEPTH, N_EXPERTS, D_MODEL, D_EXPERT), D_MODEL ** -0.5),
        'w3': nrm(ks[28], (DEPTH, N_EXPERTS, D_MODEL, D_EXPERT), D_MODEL ** -0.5),
        'w2': nrm(ks[29], (DEPTH, N_EXPERTS, D_EXPERT, D_MODEL), D_EXPERT ** -0.5 * DEEP_BETA),
        'ln2_g': 1.0 + nrm(ks[30], (DEPTH, D_MODEL), 0.02),
        'ln2_b': nrm(ks[31], (DEPTH, D_MODEL), 0.02),
    }


def reference(x, meta, ln_emb_g, ln_emb_b, w_in, conv_w, dn_a_log, dn_dt_bias, dn_norm_w, ssm_a_re, ssm_a_im,
              ssm_log_dt, ssm_b_re, ssm_b_im, ssm_c_re, ssm_c_im, ssm_d, w_glu, b_glu, w_branch, w_out,
              ln1_g, ln1_b, w_rg, b_rg, w_re, b_re, w1, w3, w2, ln2_g, ln2_b):
    bn = x.shape[0]
    h = jnp.concatenate([jnp.broadcast_to(meta[None], (bn, N_META, D_MODEL)).astype(x.dtype), x], axis=1)
    h = layer_norm(h, ln_emb_g, ln_emb_b)
    for l in range(DEPTH):
        h = hybrid_layer(h, w_in[l], conv_w[l], dn_a_log[l], dn_dt_bias[l], dn_norm_w[l], ssm_a_re[l], ssm_a_im[l],
                         ssm_log_dt[l], ssm_b_re[l], ssm_b_im[l], ssm_c_re[l], ssm_c_im[l], ssm_d[l], w_glu[l],
                         b_glu[l], w_branch[l], w_out[l], ln1_g[l], ln1_b[l], w_rg[l], b_rg[l], w_re[l], b_re[l],
                         w1[l], w3[l], w2[l], ln2_g[l], ln2_b[l])
    return h[:, N_META:]
```

```python
import functools
import math

import jax
import jax.numpy as jnp
from jax import lax
from jax.experimental import pallas as pl
from jax.experimental.pallas import tpu as pltpu

F32 = jnp.float32
BF16 = jnp.bfloat16
I32 = jnp.int32

D_MODEL = 1024
N_META = 16
HEAD_DIM = 64
DN_WIDTH = 512
DN_HEADS = 8
CONV_K = 4
CHUNK = 64
S_CH = 16
S_WIDTH = 512
S_GROUPS = 32
S_STATE = 64
S_CHAN = S_GROUPS * S_STATE
MOE_GROUPS = 8
EXPERTS_PER_GROUP = 8
N_EXPERTS = 64
D_EXPERT = 256
MOE_BLOCK = 128
DEEP_ALPHA = 2.0 ** 0.25
LN_EPS = 1e-5
NORM_EPS = 1e-6
QUAD = 4 * HEAD_DIM
GATE_W = 2 * D_MODEL
MAIN_W = 4 * DN_WIDTH + S_WIDTH + GATE_W
SEG = 64
S5_TILE = 8 * SEG
VMEM_LIMIT = 56 * 1024 * 1024


def _cparams(n_axes, semantics="arbitrary"):
    return pltpu.CompilerParams(dimension_semantics=(semantics,) * n_axes, vmem_limit_bytes=VMEM_LIMIT)


def _dot(a, b):
    return jnp.dot(a, b, preferred_element_type=F32)


def _dot_nt(a, b):
    return lax.dot_general(a, b, (((1,), (1,)), ((), ())), preferred_element_type=F32)


def _split3(x):
    hi = x.astype(BF16)
    r = x - hi.astype(F32)
    mid = r.astype(BF16)
    lo = (r - mid.astype(F32)).astype(BF16)
    return hi, mid, lo


def _dot_x3(x, m):
    hi, mid, lo = _split3(x)
    return _dot(hi, m) + _dot(mid, m) + _dot(lo, m)


def _dot_3x(m, x):
    hi, mid, lo = _split3(x)
    return _dot(m, hi) + _dot(m, mid) + _dot(m, lo)


def _layer_norm(x, g, b):
    mu = jnp.mean(x, -1, keepdims=True)
    xc = x - mu
    var = jnp.mean(xc * xc, -1, keepdims=True)
    return xc * lax.rsqrt(var + LN_EPS) * g + b


def _sigmoid(x):
    return 1.0 / (1.0 + jnp.exp(-x))


def _silu(x):
    return x * _sigmoid(x)


def _softplus(x):
    return jnp.maximum(x, 0.0) + jnp.log(1.0 + jnp.exp(-jnp.abs(x)))


def _const_spec(shape):
    nd = len(shape)
    return pl.BlockSpec(shape, lambda *_: (0,) * nd)


def _inproj_kernel(x_ref, g_ref, b_ref, wmain_ref, wab_ref, wabt_ref,
                   qkv_ref, z_ref, u_ref, gates_ref, abc_ref, abr_ref):
    hb = _layer_norm(x_ref[...], g_ref[...], b_ref[...]).astype(BF16)
    qkv_ref[...] = _dot(hb, wmain_ref[:, 0:3 * DN_WIDTH])
    z_ref[...] = _dot(hb, wmain_ref[:, 3 * DN_WIDTH:4 * DN_WIDTH])
    u_ref[...] = _dot(hb, wmain_ref[:, 4 * DN_WIDTH:4 * DN_WIDTH + S_WIDTH])
    gates_ref[...] = _dot(hb, wmain_ref[:, 4 * DN_WIDTH + S_WIDTH:MAIN_W])
    abc_ref[...] = _dot(hb, wab_ref[...])
    abr_ref[...] = _dot_nt(wabt_ref[...], hb)


def _inproj(x2, g, b, wmain, wab, wabt, tm, n_batch_tiles):
    t = x2.shape[0]
    n_tiles = t // tm
    n_b = n_tiles // n_batch_tiles
    out_shape = (
        jax.ShapeDtypeStruct((t, 3 * DN_WIDTH), F32),
        jax.ShapeDtypeStruct((t, DN_WIDTH), F32),
        jax.ShapeDtypeStruct((t // n_b, n_b * S_WIDTH), F32),
        jax.ShapeDtypeStruct((t, GATE_W), F32),
        jax.ShapeDtypeStruct((t, 2 * DN_HEADS), F32),
        jax.ShapeDtypeStruct((2 * DN_HEADS, t), F32),
    )
    row = lambda w: pl.BlockSpec((tm, w), lambda i: (i, 0))
    return pl.pallas_call(
        _inproj_kernel,
        out_shape=out_shape,
        grid=(n_tiles,),
        in_specs=[row(D_MODEL), _const_spec((1, D_MODEL)), _const_spec((1, D_MODEL)),
                  _const_spec((D_MODEL, MAIN_W)), _const_spec((D_MODEL, 2 * DN_HEADS)),
                  _const_spec((2 * DN_HEADS, D_MODEL))],
        out_specs=(row(3 * DN_WIDTH), row(DN_WIDTH),
                   pl.BlockSpec((tm, S_WIDTH), lambda i: (i % n_batch_tiles, i // n_batch_tiles)),
                   row(GATE_W), row(2 * DN_HEADS),
                   pl.BlockSpec((2 * DN_HEADS, tm), lambda i: (0, i))),
        compiler_params=_cparams(1, "parallel"),
        name="inproj",
    )(x2, g, b, wmain, wab, wabt)


def _bd(x, bdmask):
    return (jnp.concatenate([x, x, x, x], axis=0) * bdmask).astype(BF16)


def _dn_chunk(qn, kn, v, beta, gcx, grow, s_cat, bdmask, eye_cat, incl, strict):
    kb = kn * beta
    eg = jnp.exp(gcx)
    glast = gcx[CHUNK - 1:CHUNK, :]
    kd = kn * jnp.exp(glast - gcx)
    qd = qn * eg
    dec = jnp.exp(jnp.where(incl, gcx - grow, -jnp.inf))
    r1 = _dot_nt(jnp.concatenate([kb, qn], axis=0).astype(BF16), _bd(kn, bdmask))
    bm = -jnp.where(strict, r1[:CHUNK] * dec, 0.0)
    attn = r1[CHUNK:] * dec
    p = eye_cat + bm
    f = _dot(bm.astype(BF16), _bd(bm, bdmask))
    for _ in range(4):
        r = _dot(jnp.concatenate([p, f], axis=0).astype(BF16), _bd(f, bdmask))
        p = p + r[:CHUNK]
        f = r[CHUNK:]
    p = p + _dot(p.astype(BF16), _bd(f, bdmask))
    pb = p.astype(BF16)
    u = _dot(pb, _bd(v * beta, bdmask))
    w = _dot(pb, _bd(kb * eg, bdmask))
    ubd = _bd(u, bdmask)
    wbd = _bd(w, bdmask)
    ab = attn.astype(BF16)
    local = _dot(ab, ubd)
    qt = qd - _dot(ab, wbd)
    kdt = _dot_nt(eye_cat.astype(BF16), _bd(kd, bdmask)).astype(BF16)
    n_cat = _dot(kdt, ubd)
    m_cat = eye_cat * jnp.exp(glast) - _dot(kdt, wbd)
    r2 = _dot(jnp.concatenate([m_cat, qt], axis=0).astype(BF16), _bd(s_cat, bdmask))
    return r2[CHUNK:] + local, r2[:CHUNK] + n_cat


def _deltanet_kernel(qkv_ref, z_ref, abc_ref, ar_ref, s0_ref, tail0_ref,
                     convw_ref, pcol_ref, prow_ref, normw_ref, ones_ref, eg_ref, eb_ref,
                     tri_ref, triu_ref, bdmask_ref, eye_ref,
                     o_ref, send_ref, tailend_ref,
                     s_scr, tail_scr, o_scr, *, n_chunks, n_pad):
    rows = n_chunks * CHUNK
    step = pl.program_id(1)

    @pl.when(step == 0)
    def _():
        s_scr[...] = s0_ref[...]
        tail_scr[...] = tail0_ref[...]

    x = qkv_ref[...]
    prev = tail_scr[...]
    tail_scr[...] = x[rows - 8:rows, :]
    sub8 = lax.broadcasted_iota(I32, (8, 3 * DN_WIDTH), 0)
    conv = x * convw_ref[CONV_K - 1:CONV_K, :]
    for sh in range(1, CONV_K):
        rolled = pltpu.roll(x, sh, 0)
        head = jnp.where(sub8 < sh, pltpu.roll(prev, sh, 0), rolled[0:8, :])
        shifted = jnp.concatenate([head, rolled[8:, :]], axis=0)
        conv = conv + shifted * convw_ref[CONV_K - 1 - sh:CONV_K - sh, :]
    act = _silu(conv)
    q = act[:, 0:DN_WIDTH]
    k = act[:, DN_WIDTH:2 * DN_WIDTH]
    v = act[:, 2 * DN_WIDTH:3 * DN_WIDTH]
    ones_bd = ones_ref[...]
    qn = q * lax.rsqrt(_dot((q * q).astype(BF16), ones_bd) + NORM_EPS) * (HEAD_DIM ** -0.5)
    kn = k * lax.rsqrt(_dot((k * k).astype(BF16), ones_bd) + NORM_EPS)

    abc = abc_ref[...]
    g16 = -jnp.exp(pcol_ref[0:1, :]) * _softplus(abc + pcol_ref[1:2, :])
    b16 = _sigmoid(abc)
    ar = ar_ref[0]
    g_row = -jnp.exp(jnp.concatenate([prow_ref[0:2, :]] * n_chunks, axis=0)) * _softplus(
        ar + jnp.concatenate([prow_ref[2:4, :]] * n_chunks, axis=0))
    if n_pad:
        rowi = lax.broadcasted_iota(I32, (rows, 2 * DN_HEADS), 0)
        g16 = jnp.where(rowi >= n_pad, g16, 0.0)
        b16 = jnp.where(rowi >= n_pad, b16, 0.0)
        lanej = lax.broadcasted_iota(I32, g_row.shape, 1) & (CHUNK - 1)
        g_row = jnp.where(lanej >= n_pad, g_row, 0.0)
    gc16 = _dot_3x(tri_ref[...], g16)
    gcx = _dot_x3(gc16, eg_ref[...])
    beta = _dot_x3(b16, eb_ref[...])
    gc_row = _dot_x3(g_row, triu_ref[...])

    bdmask = bdmask_ref[...]
    eye_cat = eye_ref[...]
    ii = lax.broadcasted_iota(I32, (CHUNK, QUAD), 0)
    jj = lax.broadcasted_iota(I32, (CHUNK, QUAD), 1) & (CHUNK - 1)
    incl = ii >= jj
    strict = ii > jj
    for qd in range(2):
        s_cat = s_scr[qd]
        lanes = slice(qd * QUAD, (qd + 1) * QUAD)
        for c in range(n_chunks):
            rs = slice(c * CHUNK, (c + 1) * CHUNK)
            out, s_cat = _dn_chunk(qn[rs, lanes], kn[rs, lanes], v[rs, lanes], beta[rs, lanes], gcx[rs, lanes],
                                   gc_row[2 * c + qd:2 * c + qd + 1, :], s_cat, bdmask, eye_cat, incl, strict)
            o_scr[rs, lanes] = out
        s_scr[qd] = s_cat
    o = o_scr[...]
    o = o * lax.rsqrt(_dot((o * o).astype(BF16), ones_bd) * (1.0 / HEAD_DIM) + NORM_EPS) * normw_ref[...]
    o_ref[...] = o * _silu(z_ref[...])
    send_ref[...] = s_scr[...]
    tailend_ref[...] = tail_scr[...]


def _dn_consts(rows):
    lane = jnp.arange(DN_WIDTH)
    ones_bd = (lane[:, None] // HEAD_DIM == lane[None, :] // HEAD_DIM).astype(BF16)
    h16 = jnp.arange(2 * DN_HEADS)
    eg = (h16[:, None] == lane[None, :] // HEAD_DIM).astype(BF16)
    eb = (h16[:, None] - DN_HEADS == lane[None, :] // HEAD_DIM).astype(BF16)
    r = jnp.arange(rows)
    tri = ((r[:, None] // CHUNK == r[None, :] // CHUNK) & (r[:, None] >= r[None, :])).astype(BF16)
    c = jnp.arange(QUAD)
    triu = ((c[:, None] // CHUNK == c[None, :] // CHUNK) & (c[:, None] <= c[None, :])).astype(BF16)
    bdmask = (c[:, None] // CHUNK == c[None, :] // CHUNK).astype(F32)
    eye_cat = (jnp.arange(CHUNK)[:, None] == c[None, :] % CHUNK).astype(F32)
    return ones_bd, eg, eb, tri, triu, bdmask, eye_cat


def _deltanet(qkv, z, abc, ar, s0, tail0, convw, pcol, prow, normw, n_batch, n_chunks, n_pad):
    rows = n_chunks * CHUNK
    total = qkv.shape[0]
    steps = total // n_batch // rows
    consts = _dn_consts(rows)
    ar = ar.reshape(n_batch * steps, n_chunks * 2, QUAD)
    rowspec = lambda w: pl.BlockSpec((rows, w), lambda b, s: (b * steps + s, 0))
    rf = pl.BlockSpec((1, n_chunks * 2, QUAD), lambda b, s: (b * steps + s, 0, 0))
    kern = functools.partial(_deltanet_kernel, n_chunks=n_chunks, n_pad=n_pad)
    return pl.pallas_call(
        kern,
        out_shape=(jax.ShapeDtypeStruct((total, DN_WIDTH), F32),
                   jax.ShapeDtypeStruct((n_batch, 2, HEAD_DIM, QUAD), F32),
                   jax.ShapeDtypeStruct((n_batch, 8, 3 * DN_WIDTH), F32)),
        grid=(n_batch, steps),
        in_specs=[rowspec(3 * DN_WIDTH), rowspec(DN_WIDTH), rowspec(2 * DN_HEADS), rf,
                  _const_spec((2, HEAD_DIM, QUAD)), _const_spec((8, 3 * DN_WIDTH)),
                  _const_spec((CONV_K, 3 * DN_WIDTH)), _const_spec((2, 2 * DN_HEADS)), _const_spec((4, QUAD)),
                  _const_spec((1, DN_WIDTH))] + [_const_spec(c.shape) for c in consts],
        out_specs=(rowspec(DN_WIDTH),
                   pl.BlockSpec((None, 2, HEAD_DIM, QUAD), lambda b, s: (b, 0, 0, 0)),
                   pl.BlockSpec((None, 8, 3 * DN_WIDTH), lambda b, s: (b, 0, 0))),
        scratch_shapes=[pltpu.VMEM((2, HEAD_DIM, QUAD), F32), pltpu.VMEM((8, 3 * DN_WIDTH), F32),
                        pltpu.VMEM((rows, DN_WIDTH), F32)],
        compiler_params=_cparams(2),
        name="deltanet_pad%d" % n_pad,
    )(qkv, z, abc, ar, s0, tail0, convw, pcol, prow, normw, *consts)


def _row_form(a_rows):
    nc = a_rows.shape[1] // CHUNK
    x = a_rows.reshape(2, 4, nc, CHUNK)
    return jnp.transpose(x, (2, 0, 1, 3)).reshape(nc * 2, QUAD)


def _s5prep_kernel(ldt_ref, are_ref, aim_ref, bre_ref, bim_ref, cre_ref, cim_ref,
                   wb_ref, wct_ref, apre_ref, apim_ref):
    dt = jnp.exp(ldt_ref[...])
    ar = are_ref[...]
    ai = aim_ref[...]
    lre = dt * ar
    lim = dt * ai
    mag = jnp.exp(lre)
    abar_re = mag * jnp.cos(lim)
    abar_im = mag * jnp.sin(lim)
    den = ar * ar + ai * ai
    nr = abar_re - 1.0
    ni = abar_im
    f_re = (nr * ar + ni * ai) / den
    f_im = (ni * ar - nr * ai) / den
    bb_re = f_re * bre_ref[...] - f_im * bim_ref[...]
    bb_im = f_re * bim_ref[...] + f_im * bre_ref[...]
    half = S_CHAN // 2
    grp_rows = lax.broadcasted_iota(I32, (16 * S_CH, half), 0) >> int(math.log2(S_CH))
    grp_cols = lax.broadcasted_iota(I32, (16 * S_CH, half), 1) >> int(math.log2(S_STATE))
    diag = grp_rows == grp_cols
    for kt in range(2):
        cols = slice(kt * half, (kt + 1) * half)

        def blockdiag(m):
            return jnp.where(diag, jnp.concatenate([m[:, cols]] * 16, axis=0), 0.0).astype(BF16)

        wb_ref[kt, :, 0:half] = blockdiag(bb_re)
        wb_ref[kt, :, half:2 * half] = blockdiag(bb_im)
        wct_ref[kt, :, 0:half] = blockdiag(cre_ref[...])
        wct_ref[kt, :, half:2 * half] = blockdiag(-cim_ref[...])
    kk = (lax.broadcasted_iota(I32, (SEG, S_CHAN), 0) + 1).astype(F32)
    pm = jnp.exp(kk * lre)
    apre_ref[...] = pm * jnp.cos(kk * lim)
    apim_ref[...] = pm * jnp.sin(kk * lim)


def _s5prep(ldt, are, aim, bre, bim, cre, cim):
    return pl.pallas_call(
        _s5prep_kernel,
        out_shape=(jax.ShapeDtypeStruct((2, 16 * S_CH, S_CHAN), BF16),
                   jax.ShapeDtypeStruct((2, 16 * S_CH, S_CHAN), BF16),
                   jax.ShapeDtypeStruct((SEG, S_CHAN), F32),
                   jax.ShapeDtypeStruct((SEG, S_CHAN), F32)),
        compiler_params=pltpu.CompilerParams(vmem_limit_bytes=VMEM_LIMIT),
        name="s5prep",
    )(ldt, are, aim, bre, bim, cre, cim)


def _s5_kernel(u_ref, h0_ref, wb_ref, wct_ref, apre_ref, apim_ref, dskip_ref, wglu_ref, bglu_ref,
               perm_ref, permt_ref, ob_ref, hend_ref, h_scr, st_scr):
    half = S_CHAN // 2

    @pl.when(pl.program_id(1) == 0)
    def _():
        st_scr[...] = h0_ref[...]

    u = u_ref[...]
    up = _dot(perm_ref[...], u.astype(BF16)).astype(BF16)
    for kt in range(2):
        h_scr[kt] = _dot(up[:, kt * 256:(kt + 1) * 256], wb_ref[kt])

    blk = 512
    for kt in range(2):
        for cb in range(half // blk):
            re_c = slice(cb * blk, (cb + 1) * blk)
            im_c = slice(half + cb * blk, half + (cb + 1) * blk)
            ch = slice(kt * half + cb * blk, kt * half + (cb + 1) * blk)
            a_re = jnp.broadcast_to(apre_ref[0:1, ch], (8, blk))
            a_im = jnp.broadcast_to(apim_ref[0:1, ch], (8, blk))
            hr = jnp.zeros((8, blk), F32)
            hi = jnp.zeros((8, blk), F32)
            for i in range(SEG):
                rs = slice(8 * i, 8 * i + 8)
                nr = a_re * hr - a_im * hi + h_scr[kt, rs, re_c]
                ni = a_re * hi + a_im * hr + h_scr[kt, rs, im_c]
                hr, hi = nr, ni
                h_scr[kt, rs, re_c] = hr
                h_scr[kt, rs, im_c] = hi
            p_re = apre_ref[SEG - 1:SEG, ch]
            p_im = apim_ref[SEG - 1:SEG, ch]
            c_re = st_scr[0:1, ch]
            c_im = st_scr[1:2, ch]
            cs_re = [c_re]
            cs_im = [c_im]
            for j in range(8):
                e_re = hr[j:j + 1, :]
                e_im = hi[j:j + 1, :]
                n_re = p_re * c_re - p_im * c_im + e_re
                n_im = p_re * c_im + p_im * c_re + e_im
                c_re, c_im = n_re, n_im
                if j < 7:
                    cs_re.append(c_re)
                    cs_im.append(c_im)
            st_scr[0:1, ch] = c_re
            st_scr[1:2, ch] = c_im
            cm_re = jnp.concatenate(cs_re, axis=0)
            cm_im = jnp.concatenate(cs_im, axis=0)
            for i in range(SEG):
                rs = slice(8 * i, 8 * i + 8)
                w_re = apre_ref[i:i + 1, ch]
                w_im = apim_ref[i:i + 1, ch]
                h_scr[kt, rs, re_c] = h_scr[kt, rs, re_c] + (w_re * cm_re - w_im * cm_im)
                h_scr[kt, rs, im_c] = h_scr[kt, rs, im_c] + (w_re * cm_im + w_im * cm_re)
    yp = jnp.concatenate([_dot_nt(h_scr[kt].astype(BF16), wct_ref[kt]) for kt in range(2)], axis=1)
    hi_, mid_, lo_ = _split3(yp)
    pt = permt_ref[...]
    y = _dot(pt, hi_) + _dot(pt, mid_) + _dot(pt, lo_) + dskip_ref[...] * u
    yb = jax.nn.gelu(y)
    ob_ref[...] = yb * _sigmoid(_dot(yb.astype(BF16), wglu_ref[...]) + bglu_ref[...])
    hend_ref[...] = st_scr[...]


def _s5(u_tm, h0, wb, wct, apre, apim, dskip, wglu, bglu, n_batch):
    seq = u_tm.shape[0]
    steps = seq // S5_TILE
    r = jnp.arange(S5_TILE)
    perm = ((r[None, :] == (r[:, None] % 8) * SEG + r[:, None] // 8)).astype(BF16)
    permt = perm.T
    tile = pl.BlockSpec((S5_TILE, S_WIDTH), lambda b, s: (s, b))
    return pl.pallas_call(
        _s5_kernel,
        out_shape=(jax.ShapeDtypeStruct((seq, n_batch * S_WIDTH), F32),
                   jax.ShapeDtypeStruct((n_batch, 2, S_CHAN), F32)),
        grid=(n_batch, steps),
        in_specs=[tile, _const_spec((2, S_CHAN)), _const_spec(wb.shape), _const_spec(wct.shape),
                  _const_spec((SEG, S_CHAN)), _const_spec((SEG, S_CHAN)), _const_spec((1, S_WIDTH)),
                  _const_spec((S_WIDTH, S_WIDTH)), _const_spec((1, S_WIDTH)),
                  _const_spec((S5_TILE, S5_TILE)), _const_spec((S5_TILE, S5_TILE))],
        out_specs=(tile, pl.BlockSpec((None, 2, S_CHAN), lambda b, s: (b, 0, 0))),
        scratch_shapes=[pltpu.VMEM((2, S5_TILE, S_CHAN), F32), pltpu.VMEM((2, S_CHAN), F32)],
        compiler_params=_cparams(2),
        name="s5",
    )(u_tm, h0, wb, wct, apre, apim, dskip, wglu, bglu, perm, permt)


def _mixers(x, meta, ln_emb_g, ln_emb_b, w_in, conv_w, dn_a_log, dn_dt_bias, dn_norm_w, ssm_a_re, ssm_a_im,
            ssm_log_dt, ssm_b_re, ssm_b_im, ssm_c_re, ssm_c_im, ssm_d, w_glu, b_glu, dn_chunks=4):
    n_b, seq, _ = x.shape
    x2 = x.reshape(n_b * seq, D_MODEL)
    g = ln_emb_g.reshape(1, D_MODEL)
    b = ln_emb_b.reshape(1, D_MODEL)
    wmain = w_in[:, :MAIN_W].astype(BF16)
    wab = w_in[:, MAIN_W:].astype(BF16)
    wabt = wab.T
    tm = 512
    qkv, z, u_tm, gates, abc, abr = _inproj(x2, g, b, wmain, wab, wabt, tm, seq // tm)
    qkv_m, _, u_m, _, abc_m, abr_m = _inproj(meta, g, b, wmain, wab, wabt, N_META, 1)

    pad = CHUNK - N_META
    convw = conv_w
    pcol = jnp.stack([jnp.pad(dn_a_log, (0, DN_HEADS)), jnp.pad(dn_dt_bias, (0, DN_HEADS))])
    prow = jnp.concatenate([jnp.repeat(dn_a_log, CHUNK).reshape(2, QUAD),
                            jnp.repeat(dn_dt_bias, CHUNK).reshape(2, QUAD)], axis=0)
    normw = jnp.tile(dn_norm_w, DN_HEADS).reshape(1, DN_WIDTH)
    zero_s = jnp.zeros((2, HEAD_DIM, QUAD), F32)
    zero_tail = jnp.zeros((8, 3 * DN_WIDTH), F32)
    front = lambda a: jnp.pad(a, ((pad, 0), (0, 0)))
    ar_m = _row_form(jnp.pad(abr_m[:DN_HEADS], ((0, 0), (pad, 0))))
    _, s0, tail0 = _deltanet(front(qkv_m), jnp.zeros((CHUNK, DN_WIDTH), F32), front(abc_m), ar_m, zero_s, zero_tail,
                             convw, pcol, prow, normw, 1, 1, pad)
    o_a, _, _ = _deltanet(qkv, z, abc, _row_form(abr[:DN_HEADS]), s0[0], tail0[0],
                          convw, pcol, prow, normw, n_b, dn_chunks, 0)

    flat = lambda a: a.reshape(1, S_CHAN)
    chan = lambda a, perm: jnp.transpose(a, perm).reshape(S_CH, S_CHAN)
    wb, wct, apre, apim = _s5prep(flat(jnp.repeat(ssm_log_dt, S_STATE)), flat(ssm_a_re), flat(ssm_a_im),
                                  chan(ssm_b_re, (2, 0, 1)), chan(ssm_b_im, (2, 0, 1)),
                                  chan(ssm_c_re, (1, 0, 2)), chan(ssm_c_im, (1, 0, 2)))
    dskip = ssm_d.reshape(1, S_WIDTH)
    wglu = w_glu.astype(BF16)
    bglu = b_glu.reshape(1, S_WIDTH)
    u_m_tile = jnp.pad(u_m, ((S5_TILE - N_META, 0), (0, 0)))
    _, h0 = _s5(u_m_tile, jnp.zeros((2, S_CHAN), F32), wb, wct, apre, apim, dskip, wglu, bglu, 1)
    o_b, _ = _s5(u_tm, h0[0], wb, wct, apre, apim, dskip, wglu, bglu, n_b)
    return o_a, o_b, gates


def _merge_kernel(x_ref, oa_ref, ob_ref, gates_ref, lng_ref, lnb_ref, wb0_ref, wb1_ref, wout_ref,
                  ln1g_ref, ln1b_ref, wrt_ref, brt_ref, h1_ref, ids_ref, gw_ref):
    tm = x_ref.shape[0]
    h0 = _layer_norm(x_ref[...], lng_ref[...], lnb_ref[...])
    gates = gates_ref[...]
    merged = (_sigmoid(gates[:, :D_MODEL]) * _dot(oa_ref[...].astype(BF16), wb0_ref[...])
              + _sigmoid(gates[:, D_MODEL:]) * _dot(ob_ref[...].astype(BF16), wb1_ref[...]))
    h1 = _layer_norm(DEEP_ALPHA * h0 + _dot(merged.astype(BF16), wout_ref[...]), ln1g_ref[...], ln1b_ref[...])
    h1_ref[...] = h1
    w_hi, w_mid, _ = _split3(wrt_ref[...])
    h_hi, h_mid, _ = _split3(h1)
    logits = _dot_nt(w_hi, h_hi) + _dot_nt(w_hi, h_mid) + _dot_nt(w_mid, h_hi) + brt_ref[...]
    gl = logits[0:MOE_GROUPS, :]
    el = logits[MOE_GROUPS:MOE_GROUPS + N_EXPERTS, :]
    gmax = jnp.max(gl, axis=0, keepdims=True)
    rg = lax.broadcasted_iota(I32, (MOE_GROUPS, tm), 0).astype(F32)
    gidx = jnp.min(jnp.where(gl == gmax, rg, float(MOE_GROUPS)), axis=0, keepdims=True)
    gp = 1.0 / jnp.sum(jnp.exp(gl - gmax), axis=0, keepdims=True)
    rei = lax.broadcasted_iota(I32, (N_EXPERTS, tm), 0)
    re = rei.astype(F32)
    m1 = jnp.where((rei >> int(math.log2(EXPERTS_PER_GROUP))).astype(F32) == gidx, el, -jnp.inf)
    t1 = jnp.max(m1, axis=0, keepdims=True)
    i1 = jnp.min(jnp.where(m1 == t1, re, float(N_EXPERTS)), axis=0, keepdims=True)
    m2 = jnp.where(re == i1, -jnp.inf, m1)
    t2 = jnp.max(m2, axis=0, keepdims=True)
    i2 = jnp.min(jnp.where(m2 == t2, re, float(N_EXPERTS)), axis=0, keepdims=True)
    e21 = jnp.exp(t2 - t1)
    p1 = gp / (1.0 + e21)
    zeros = jnp.zeros((6, tm), F32)
    ids_ref[...] = jnp.concatenate([i1, i2, zeros], axis=0).astype(I32)
    gw_ref[...] = jnp.concatenate([p1, p1 * e21, zeros], axis=0)


def _merge(x2, o_a, o_b_tm, gates, lng, lnb, wb0, wb1, wout, ln1g, ln1b, wrt, brt, tm, n_batch_tiles):
    t = x2.shape[0]
    row = lambda w: pl.BlockSpec((tm, w), lambda i: (i, 0))
    return pl.pallas_call(
        _merge_kernel,
        out_shape=(jax.ShapeDtypeStruct((t, D_MODEL), F32),
                   jax.ShapeDtypeStruct((8, t), I32),
                   jax.ShapeDtypeStruct((8, t), F32)),
        grid=(t // tm,),
        in_specs=[row(D_MODEL), row(DN_WIDTH),
                  pl.BlockSpec((tm, S_WIDTH), lambda i: (i % n_batch_tiles, i // n_batch_tiles)),
                  row(GATE_W), _const_spec((1, D_MODEL)), _const_spec((1, D_MODEL)),
                  _const_spec((DN_WIDTH, D_MODEL)), _const_spec((S_WIDTH, D_MODEL)), _const_spec((D_MODEL, D_MODEL)),
                  _const_spec((1, D_MODEL)), _const_spec((1, D_MODEL)),
                  _const_spec((128, D_MODEL)), _const_spec((128, 1))],
        out_specs=(row(D_MODEL), pl.BlockSpec((8, tm), lambda i: (0, i)), pl.BlockSpec((8, tm), lambda i: (0, i))),
        compiler_params=_cparams(1, "parallel"),
        name="merge",
    )(x2, o_a, o_b_tm, gates, lng, lnb, wb0, wb1, wout, ln1g, ln1b, wrt, brt)


def _moepos_kernel(ids_ref, ustrict_ref, lstrict_ref, pos_ref, cnt_ref, rank_scr, *, tile):
    n_tok = ids_ref.shape[1]
    n_tiles = n_tok // tile
    re = lax.broadcasted_iota(I32, (N_EXPERTS, tile), 0)
    ustrict = ustrict_ref[...]

    def count(c, base):
        k = c // n_tiles
        off = pl.multiple_of((c % n_tiles) * tile, tile)
        oh = re == ids_ref[pl.ds(k, 1), pl.ds(off, tile)]
        ohf = oh.astype(F32)
        cum = _dot(ohf.astype(BF16), ustrict)
        rank_scr[pl.ds(k, 1), pl.ds(off, tile)] = jnp.sum(jnp.where(oh, cum + base, 0.0), axis=0, keepdims=True)
        return base + jnp.sum(ohf, axis=1, keepdims=True)

    counts = lax.fori_loop(0, 2 * n_tiles, count, jnp.zeros((N_EXPERTS, 1), F32))
    starts = _dot_3x(lstrict_ref[...], jnp.broadcast_to(counts, (N_EXPERTS, 128)))[:, 0:1]
    cnt_ref[...] = jnp.broadcast_to(counts, (N_EXPERTS, 128)).astype(I32)

    def place(c, carry):
        k = c // n_tiles
        off = pl.multiple_of((c % n_tiles) * tile, tile)
        oh = re == ids_ref[pl.ds(k, 1), pl.ds(off, tile)]
        start = jnp.sum(jnp.where(oh, starts, 0.0), axis=0, keepdims=True)
        pos_ref[pl.ds(k, 1), pl.ds(off, tile)] = (rank_scr[pl.ds(k, 1), pl.ds(off, tile)] + start).astype(I32)
        return carry

    lax.fori_loop(0, 2 * n_tiles, place, 0)


def _moepos(ids, tile=512):
    n_tok = ids.shape[1]
    r = jnp.arange(tile)
    ustrict = (r[:, None] < r[None, :]).astype(BF16)
    e = jnp.arange(N_EXPERTS)
    lstrict = (e[:, None] > e[None, :]).astype(BF16)
    return pl.pallas_call(
        functools.partial(_moepos_kernel, tile=tile),
        out_shape=(jax.ShapeDtypeStruct((2, n_tok), I32), jax.ShapeDtypeStruct((N_EXPERTS, 128), I32)),
        scratch_shapes=[pltpu.VMEM((2, n_tok), F32)],
        compiler_params=pltpu.CompilerParams(vmem_limit_bytes=VMEM_LIMIT),
        name="moepos",
    )(ids, ustrict, lstrict)


def _row_copy_wait(ref_a, ref_b, sem, n):
    def body(_, c):
        pltpu.make_async_copy(ref_a, ref_b, sem).wait()
        return c
    lax.fori_loop(0, n, body, 0)


def _dispatch_kernel(pos_ref, h_ref, xs_ref, sem):
    tm = h_ref.shape[0]

    def issue(r, c):
        for k in range(2):
            pltpu.make_async_copy(h_ref.at[pl.ds(r, 1), :], xs_ref.at[pl.ds(pos_ref[k, r], 1), :], sem).start()
        return c

    lax.fori_loop(0, tm, issue, 0)
    _row_copy_wait(h_ref.at[pl.ds(0, 1), :], xs_ref.at[pl.ds(0, 1), :], sem, 2 * tm)


def _dispatch(pos, h1, tm):
    t = h1.shape[0]
    return pl.pallas_call(
        _dispatch_kernel,
        out_shape=jax.ShapeDtypeStruct((2 * t, D_MODEL), F32),
        grid=(t // tm,),
        in_specs=[pl.BlockSpec((2, tm), lambda i: (0, i), memory_space=pltpu.SMEM),
                  pl.BlockSpec((tm, D_MODEL), lambda i: (i, 0))],
        out_specs=pl.BlockSpec(memory_space=pl.ANY),
        scratch_shapes=[pltpu.SemaphoreType.DMA],
        compiler_params=_cparams(1),
        name="dispatch",
    )(pos, h1)


def _experts_kernel(blk_ref, exp_ref, lo_ref, hi_ref, x_ref, w1_ref, w3_ref, w2_ref, y_ref):
    i = pl.program_id(0)
    lo = lo_ref[i]
    hi = hi_ref[i]

    @pl.when(hi > lo)
    def _():
        x = x_ref[...].astype(BF16)
        hid = _silu(_dot(x, w1_ref[...].astype(BF16))) * _dot(x, w3_ref[...].astype(BF16))
        y = _dot(hid.astype(BF16), w2_ref[...].astype(BF16))
        rows = lax.broadcasted_iota(I32, y.shape, 0)
        y = jnp.where((rows >= lo) & (rows < hi), y, 0.0)

        @pl.when(lo == 0)
        def _():
            y_ref[...] = y

        @pl.when(lo > 0)
        def _():
            y_ref[...] = y_ref[...] + y


def _experts(item_blk, item_exp, item_lo, item_hi, xs, w1, w3, w2):
    n_items = item_blk.shape[0]
    grid_spec = pltpu.PrefetchScalarGridSpec(
        num_scalar_prefetch=4,
        grid=(n_items,),
        in_specs=[pl.BlockSpec((MOE_BLOCK, D_MODEL), lambda i, b, e, lo, hi: (b[i], 0)),
                  pl.BlockSpec((None, D_MODEL, D_EXPERT), lambda i, b, e, lo, hi: (e[i], 0, 0)),
                  pl.BlockSpec((None, D_MODEL, D_EXPERT), lambda i, b, e, lo, hi: (e[i], 0, 0)),
                  pl.BlockSpec((None, D_EXPERT, D_MODEL), lambda i, b, e, lo, hi: (e[i], 0, 0))],
        out_specs=pl.BlockSpec((MOE_BLOCK, D_MODEL), lambda i, b, e, lo, hi: (b[i], 0)),
    )
    return pl.pallas_call(
        _experts_kernel,
        out_shape=jax.ShapeDtypeStruct(xs.shape, F32),
        grid_spec=grid_spec,
        compiler_params=_cparams(1),
        name="experts",
    )(item_blk, item_exp, item_lo, item_hi, xs, w1, w3, w2)


def _combine_kernel(pos_ref, gw_ref, h_ref, eye_ref, lng_ref, lnb_ref, ys_ref, o_ref, yg_scr, sem):
    tm = h_ref.shape[0]

    def issue(r, c):
        for k in range(2):
            pltpu.make_async_copy(ys_ref.at[pl.ds(pos_ref[k, r], 1), :], yg_scr.at[k, pl.ds(r, 1), :], sem).start()
        return c

    lax.fori_loop(0, tm, issue, 0)
    g_hi, g_mid, g_lo = _split3(gw_ref[...])
    eye = eye_ref[...]
    gcol = _dot_nt(eye, g_hi) + _dot_nt(eye, g_mid) + _dot_nt(eye, g_lo)
    _row_copy_wait(ys_ref.at[pl.ds(0, 1), :], yg_scr.at[0, pl.ds(0, 1), :], sem, 2 * tm)
    moe = gcol[:, 0:1] * yg_scr[0] + gcol[:, 1:2] * yg_scr[1]
    o_ref[...] = _layer_norm(DEEP_ALPHA * h_ref[...] + moe, lng_ref[...], lnb_ref[...])


def _combine(pos, gw, h1, ys, lng, lnb, tm):
    t = h1.shape[0]
    eye = jnp.eye(tm, dtype=BF16)
    return pl.pallas_call(
        _combine_kernel,
        out_shape=jax.ShapeDtypeStruct((t, D_MODEL), F32),
        grid=(t // tm,),
        in_specs=[pl.BlockSpec((2, tm), lambda i: (0, i), memory_space=pltpu.SMEM),
                  pl.BlockSpec((8, tm), lambda i: (0, i)),
                  pl.BlockSpec((tm, D_MODEL), lambda i: (i, 0)),
                  _const_spec((tm, tm)), _const_spec((1, D_MODEL)), _const_spec((1, D_MODEL)),
                  pl.BlockSpec(memory_space=pl.ANY)],
        out_specs=pl.BlockSpec((tm, D_MODEL), lambda i: (i, 0)),
        scratch_shapes=[pltpu.VMEM((2, tm, D_MODEL), F32), pltpu.SemaphoreType.DMA],
        compiler_params=_cparams(1),
        name="combine",
    )(pos, gw, h1, eye, lng, lnb, ys)


def _expert_schedule(counts, n_blocks):
    n_items = n_blocks + N_EXPERTS
    en = jnp.cumsum(counts)
    st = en - counts
    first = st // MOE_BLOCK
    n_e = jnp.where(counts > 0, (en - 1) // MOE_BLOCK - first + 1, 0)
    item_end = jnp.cumsum(n_e)
    item_off = item_end - n_e
    k = jnp.arange(n_items, dtype=I32)
    e = jnp.minimum(jnp.sum(item_end[None, :] <= k[:, None], axis=1), N_EXPERTS - 1).astype(I32)
    valid = k < item_end[-1]
    last_e = jnp.max(jnp.where(counts > 0, jnp.arange(N_EXPERTS, dtype=I32), 0))
    e = jnp.where(valid, e, last_e)
    blk = jnp.where(valid, first[e] + k - item_off[e], n_blocks - 1).astype(I32)
    lo = jnp.where(valid, jnp.maximum(st[e], blk * MOE_BLOCK) - blk * MOE_BLOCK, 0).astype(I32)
    hi = jnp.where(valid, jnp.minimum(en[e], (blk + 1) * MOE_BLOCK) - blk * MOE_BLOCK, 0).astype(I32)
    return blk, e, lo, hi


def kernel(x, meta, ln_emb_g, ln_emb_b, w_in, conv_w, dn_a_log, dn_dt_bias, dn_norm_w, ssm_a_re, ssm_a_im, ssm_log_dt, ssm_b_re, ssm_b_im, ssm_c_re, ssm_c_im, ssm_d, w_glu, b_glu, w_branch, w_out, ln1_g, ln1_b, w_rg, b_rg, w_re, b_re, w1, w3, w2, ln2_g, ln2_b):
    assert w_in.shape[0] == 1, "single-layer block"
    n_b, seq, _ = x.shape
    t = n_b * seq
    o_a, o_b, gates = _mixers(x, meta, ln_emb_g, ln_emb_b, w_in[0], conv_w[0], dn_a_log[0], dn_dt_bias[0],
                              dn_norm_w[0], ssm_a_re[0], ssm_a_im[0], ssm_log_dt[0], ssm_b_re[0], ssm_b_im[0],
                              ssm_c_re[0], ssm_c_im[0], ssm_d[0], w_glu[0], b_glu[0])
    row = lambda a: a.reshape(1, D_MODEL)
    wrt = jnp.pad(jnp.concatenate([w_rg[0], w_re[0]], axis=1).T, ((0, 128 - MOE_GROUPS - N_EXPERTS), (0, 0)))
    brt = jnp.pad(jnp.concatenate([b_rg[0], b_re[0]]), (0, 128 - MOE_GROUPS - N_EXPERTS)).reshape(128, 1)
    tm = 512
    h1, ids, gw = _merge(x.reshape(t, D_MODEL), o_a, o_b, gates, row(ln_emb_g), row(ln_emb_b),
                         w_branch[0, 0].astype(BF16), w_branch[0, 1].astype(BF16), w_out[0].astype(BF16),
                         row(ln1_g[0]), row(ln1_b[0]), wrt, brt, tm, seq // tm)
    pos, counts = _moepos(ids)
    n_blocks = 2 * t // MOE_BLOCK
    blk, e, lo, hi = _expert_schedule(counts[:, 0], n_blocks)
    xs = _dispatch(pos, h1, tm)
    ys = _experts(blk, e, lo, hi, xs, w1[0], w3[0], w2[0])
    out = _combine(pos, gw, h1, ys, row(ln2_g[0]), row(ln2_b[0]), tm)
    return out.reshape(n_b, seq, D_MODEL)
```
